```python
import jax, jax.numpy as jnp
from jax import lax
import numpy as np

D_MODEL = 1024
BATCH = 16
SEQ = 2048
DEPTH = 2

N_MIXERS = 2
CONV_WIDTH = D_MODEL
CONV_K = 3
N_HEADS = 16
HEAD_DIM = D_MODEL // N_HEADS
ATTN_WIDTH = N_HEADS * HEAD_DIM
MOBA_BLOCK = 256
MOBA_TOPK = 3
Q_CHUNK = 64
N_CONV_LAYERS = (DEPTH + 1) // 2
N_ATTN_LAYERS = DEPTH // 2
EPS = 1e-6
NEG = -1e30

kernel_name = "hybrid_shortconv_moba_gated"


def rms_norm(x, g):
    x32 = x.astype(jnp.float32)
    y = x32 * lax.rsqrt(jnp.mean(x32 * x32, axis=-1, keepdims=True) + EPS)
    return (y * g.astype(jnp.float32)).astype(x.dtype)


def alibi_slopes(n_heads):
    return jnp.asarray(2.0 ** (-8.0 * np.arange(1, n_heads + 1) / n_heads), dtype=jnp.float32)


def short_conv_mixer(h, w_in, conv_w, w_out):
    S = h.shape[1]
    b_gate, c_gate, xv, z = jnp.split(h @ w_in, 4, axis=-1)
    u = c_gate * xv
    u_pad = jnp.pad(u, ((0, 0), (CONV_K - 1, 0), (0, 0)))
    conv = conv_w[0] * u_pad[:, 0:S]
    for k in range(1, CONV_K):
        conv = conv + conv_w[k] * u_pad[:, k:k + S]
    y = b_gate * conv * jax.nn.silu(z)
    return y @ w_out


def moba_mixer(h, w_in, q_gain, k_gain, w_out):
    Bsz, S, _ = h.shape
    q, k, v, z = jnp.split(h @ w_in, 4, axis=-1)

    def heads(t):
        return t.reshape(Bsz, S, N_HEADS, HEAD_DIM).transpose(0, 2, 1, 3)

    q = rms_norm(heads(q), q_gain) * (HEAD_DIM ** -0.5)
    k = rms_norm(heads(k), k_gain)
    v = heads(v)

    n_blk = -(-S // MOBA_BLOCK)
    s_pad = n_blk * MOBA_BLOCK
    pad = ((0, 0), (0, 0), (0, s_pad - S), (0, 0))
    q, k, v = jnp.pad(q, pad), jnp.pad(k, pad), jnp.pad(v, pad)
    k_blocks = k.reshape(Bsz, N_HEADS, n_blk, MOBA_BLOCK, HEAD_DIM)
    v_blocks = v.reshape(Bsz, N_HEADS, n_blk, MOBA_BLOCK, HEAD_DIM)
    k_mean = jnp.mean(k_blocks.astype(jnp.float32), axis=3).astype(k.dtype)

    topk = min(MOBA_TOPK, n_blk)
    slopes = alibi_slopes(N_HEADS)
    n_chunk = s_pad // Q_CHUNK
    h_ar = jnp.arange(N_HEADS)[:, None, None]
    blk_ar = jnp.arange(n_blk)
    key_off = jnp.arange(MOBA_BLOCK)
    q_off = jnp.arange(Q_CHUNK)

    def chunk(i):
        b = i // n_chunk
        c = i % n_chunk
        t0 = c * Q_CHUNK
        qc = lax.dynamic_slice_in_dim(q[b], t0, Q_CHUNK, axis=1)
        kb = k_blocks[b]
        vb = v_blocks[b]
        q_blk = t0 // MOBA_BLOCK
        t_pos = t0 + q_off
        gate = jnp.einsum('hqd,hnd->hqn', qc, k_mean[b]).astype(jnp.float32)
        gate = jnp.where(blk_ar < q_blk, gate, NEG)
        _, idx = lax.top_k(gate, topk)
        valid = idx < q_blk
        k_sel = kb[h_ar, idx]
        v_sel = vb[h_ar, idx]
        s_past = jnp.einsum('hqd,hqjkd->hqjk', qc, k_sel).astype(jnp.float32)
        past_dist = (t_pos[None, :, None, None] - (idx[..., None] * MOBA_BLOCK + key_off)).astype(jnp.float32)
        s_past = s_past - slopes[:, None, None, None] * past_dist
        s_past = jnp.where(valid[..., None], s_past, NEG)
        k_own = lax.dynamic_index_in_dim(kb, q_blk, axis=1, keepdims=False)
        v_own = lax.dynamic_index_in_dim(vb, q_blk, axis=1, keepdims=False)
        own_dist = t_pos[:, None] - (q_blk * MOBA_BLOCK + key_off)[None, :]
        s_own = jnp.einsum('hqd,hkd->hqk', qc, k_own).astype(jnp.float32)
        s_own = jnp.where(own_dist[None] >= 0,
                          s_own - slopes[:, None, None] * own_dist.astype(jnp.float32)[None], NEG)
        n_past = topk * MOBA_BLOCK
        scores = jnp.concatenate([s_past.reshape(N_HEADS, Q_CHUNK, n_past), s_own], axis=-1)
        p = jax.nn.softmax(scores, axis=-1).astype(v.dtype)
        p_past = p[..., :n_past].reshape(N_HEADS, Q_CHUNK, topk, MOBA_BLOCK)
        p_own = p[..., n_past:]
        return (jnp.einsum('hqjk,hqjkd->hqd', p_past, v_sel)
                + jnp.einsum('hqk,hkd->hqd', p_own, v_own))

    out = lax.map(chunk, jnp.arange(Bsz * n_chunk))
    out = out.reshape(Bsz, n_chunk, N_HEADS, Q_CHUNK, HEAD_DIM).transpose(0, 1, 3, 2, 4)
    out = out.reshape(Bsz, s_pad, ATTN_WIDTH)[:, :S]
    y = out * jax.nn.silu(z)
    return y @ w_out


def setup_inputs(seed: int = 0) -> dict:
    key = jax.random.key(seed)
    ks = jax.random.split(key, 10)
    f32 = jnp.float32
    x = jax.random.normal(ks[0], (BATCH, SEQ, D_MODEL), f32)
    norm_g = 1.0 + 0.02 * jax.random.normal(ks[1], (DEPTH, D_MODEL), f32)
    conv_w_in = jax.random.normal(ks[2], (N_CONV_LAYERS, D_MODEL, 4 * CONV_WIDTH), f32) * D_MODEL ** -0.5
    conv_w = jax.random.normal(ks[3], (N_CONV_LAYERS, CONV_K, CONV_WIDTH), f32) * CONV_K ** -0.5
    conv_w_out = jax.random.normal(ks[4], (N_CONV_LAYERS, CONV_WIDTH, D_MODEL), f32) * CONV_WIDTH ** -0.5
    attn_w_in = jax.random.normal(ks[5], (N_ATTN_LAYERS, D_MODEL, 4 * ATTN_WIDTH), f32) * D_MODEL ** -0.5
    q_norm_g = 1.0 + 0.02 * jax.random.normal(ks[6], (N_ATTN_LAYERS, HEAD_DIM), f32)
    k_norm_g = 1.0 + 0.02 * jax.random.normal(ks[7], (N_ATTN_LAYERS, HEAD_DIM), f32)
    attn_w_out = jax.random.normal(ks[8], (N_ATTN_LAYERS, ATTN_WIDTH, D_MODEL), f32) * ATTN_WIDTH ** -0.5
    return {"x": x, "norm_g": norm_g, "conv_w_in": conv_w_in, "conv_w": conv_w,
            "conv_w_out": conv_w_out, "attn_w_in": attn_w_in, "q_norm_g": q_norm_g,
            "k_norm_g": k_norm_g, "attn_w_out": attn_w_out}


def reference(x, norm_g, conv_w_in, conv_w, conv_w_out, attn_w_in, q_norm_g, k_norm_g, attn_w_out):
    h = x
    for i in range(DEPTH):
        hn = rms_norm(h, norm_g[i])
        j = i // N_MIXERS
        if i % N_MIXERS == 0:
            h = h + short_conv_mixer(hn, conv_w_in[j], conv_w[j], conv_w_out[j])
        else:
            h = h + moba_mixer(hn, attn_w_in[j], q_norm_g[j], k_norm_g[j], attn_w_out[j])
    return h
```

```python
import functools

import numpy as np
import jax
import jax.numpy as jnp
from jax import lax
from jax.experimental import pallas as pl
from jax.experimental.pallas import tpu as pltpu

D_MODEL = 1024
N_HEADS = 16
HEAD_DIM = D_MODEL // N_HEADS
CONV_K = 3
MOBA_BLOCK = 256
MOBA_TOPK = 3
EPS = 1e-6
NEG = -1e30

V7X_LANES = 128
V7X_SUBLANES = 8
HEADS_PER_STEP = V7X_LANES // HEAD_DIM
KMEAN_ROWS = 16

F32 = jnp.float32
BF16 = jnp.bfloat16

_NT = (((1,), (1,)), ((), ()))


def _rms_rows(x, g):
    ms = jnp.mean(x * x, axis=-1, keepdims=True)
    return x * lax.rsqrt(ms + EPS) * g


def _dot(a, b):
    return jnp.dot(a, b, preferred_element_type=F32)


def _conv_layer_kernel(x_ref, g_ref, win_ref, cw_ref, wout_ref, o_ref, u_scr, *, tm, ce):
    e = D_MODEL
    x = x_ref[0]
    hn = _rms_rows(x, g_ref[...]).astype(BF16)

    @pl.when(pl.program_id(1) == 0)
    def _():
        u_scr[0:V7X_SUBLANES, :] = jnp.zeros((V7X_SUBLANES, e), F32)

    acc = x
    for c in range(e // ce):
        lo = c * ce
        bg = _dot(hn, win_ref[:, lo:lo + ce])
        cg = _dot(hn, win_ref[:, e + lo:e + lo + ce])
        xv = _dot(hn, win_ref[:, 2 * e + lo:2 * e + lo + ce])
        z = _dot(hn, win_ref[:, 3 * e + lo:3 * e + lo + ce])
        u = cg * xv
        u_scr[V7X_SUBLANES:V7X_SUBLANES + tm, lo:lo + ce] = u
        um1 = u_scr[V7X_SUBLANES - 1:V7X_SUBLANES - 1 + tm, lo:lo + ce]
        um2 = u_scr[V7X_SUBLANES - 2:V7X_SUBLANES - 2 + tm, lo:lo + ce]
        conv = (cw_ref[0:1, lo:lo + ce] * um2 + cw_ref[1:2, lo:lo + ce] * um1
                + cw_ref[2:3, lo:lo + ce] * u)
        y = bg * conv * (z * jax.nn.sigmoid(z))
        acc = acc + _dot(y.astype(BF16), wout_ref[lo:lo + ce, :])
        u_scr[0:V7X_SUBLANES, lo:lo + ce] = u_scr[tm:tm + V7X_SUBLANES, lo:lo + ce]
    o_ref[0] = acc


def _conv_layer(x, g, w_in, cw, w_out, *, tm=512, ce=256):
    b, s, d = x.shape
    kern = functools.partial(_conv_layer_kernel, tm=tm, ce=ce)
    return pl.pallas_call(
        kern,
        grid=(b, s // tm),
        in_specs=[
            pl.BlockSpec((1, tm, d), lambda i, t: (i, t, 0)),
            pl.BlockSpec((1, d), lambda i, t: (0, 0)),
            pl.BlockSpec((d, 4 * d), lambda i, t: (0, 0)),
            pl.BlockSpec((CONV_K, d), lambda i, t: (0, 0)),
            pl.BlockSpec((d, d), lambda i, t: (0, 0)),
        ],
        out_specs=pl.BlockSpec((1, tm, d), lambda i, t: (i, t, 0)),
        out_shape=jax.ShapeDtypeStruct((b, s, d), F32),
        scratch_shapes=[pltpu.VMEM((tm + V7X_SUBLANES, d), F32)],
        compiler_params=pltpu.CompilerParams(
            dimension_semantics=("arbitrary", "arbitrary"),
            vmem_limit_bytes=56 * 1024 * 1024),
        name="conv_layer",
    )(x, g, w_in, cw, w_out)


def _attn_proj_kernel(h_ref, g_ref, win_ref, qg_ref, kg_ref, hm_ref,
                      q_ref, k_ref, vt_ref, z_ref, km_ref, *, tm, ce):
    e = D_MODEL
    hn = _rms_rows(h_ref[0], g_ref[...]).astype(BF16)
    hm = hm_ref[...]
    nb = tm // MOBA_BLOCK
    for c in range(e // ce):
        lo = c * ce
        q = _dot(hn, win_ref[:, lo:lo + ce])
        qms = _dot((q * q).astype(BF16), hm)
        qn = q * lax.rsqrt(qms + EPS) * qg_ref[0:1, lo:lo + ce] * (HEAD_DIM ** -0.5)
        q_ref[0, :, lo:lo + ce] = qn.astype(BF16)

        k = _dot(hn, win_ref[:, e + lo:e + lo + ce])
        kms = _dot((k * k).astype(BF16), hm)
        kn = k * lax.rsqrt(kms + EPS) * kg_ref[0:1, lo:lo + ce]
        k_ref[0, :, lo:lo + ce] = kn.astype(BF16)
        for j in range(nb):
            km_ref[0, 0, j:j + 1, lo:lo + ce] = jnp.mean(
                kn[j * MOBA_BLOCK:(j + 1) * MOBA_BLOCK, :], axis=0, keepdims=True)

        v = _dot(hn, win_ref[:, 2 * e + lo:2 * e + lo + ce])
        vt_ref[0, lo:lo + ce, :] = v.T.astype(BF16)

        z_ref[0, :, lo:lo + ce] = _dot(hn, win_ref[:, 3 * e + lo:3 * e + lo + ce])


def _attn_proj(h, g, w_in, qg, kg, hm, *, tm=512, ce=256):
    b, s, d = h.shape
    nb = tm // MOBA_BLOCK
    kern = functools.partial(_attn_proj_kernel, tm=tm, ce=ce)
    row = lambda i, t: (i, t, 0)
    const = lambda i, t: (0, 0)
    return pl.pallas_call(
        kern,
        grid=(b, s // tm),
        in_specs=[
            pl.BlockSpec((1, tm, d), row),
            pl.BlockSpec((1, d), const),
            pl.BlockSpec((d, 4 * d), const),
            pl.BlockSpec((1, d), const),
            pl.BlockSpec((1, d), const),
            pl.BlockSpec((ce, ce), const),
        ],
        out_specs=[
            pl.BlockSpec((1, tm, d), row),
            pl.BlockSpec((1, tm, d), row),
            pl.BlockSpec((1, d, tm), lambda i, t: (i, 0, t)),
            pl.BlockSpec((1, tm, d), row),
            pl.BlockSpec((1, 1, nb, d), lambda i, t: (i, t, 0, 0)),
        ],
        out_shape=[
            jax.ShapeDtypeStruct((b, s, d), BF16),
            jax.ShapeDtypeStruct((b, s, d), BF16),
            jax.ShapeDtypeStruct((b, d, s), BF16),
            jax.ShapeDtypeStruct((b, s, d), F32),
            jax.ShapeDtypeStruct((b, s // tm, nb, d), F32),
        ],
        compiler_params=pltpu.CompilerParams(
            dimension_semantics=("arbitrary", "arbitrary"),
            vmem_limit_bytes=56 * 1024 * 1024),
        name="attn_proj",
    )(h, g, w_in, qg, kg, hm)


def _moba_kernel(slope_ref, q_ref, k_ref, vt_ref, km_ref, o_ref, *, n_blk):
    blk = MOBA_BLOCK
    hp = pl.program_id(1)
    km = km_ref[0].astype(BF16)
    lane = lax.broadcasted_iota(jnp.int32, (blk, V7X_LANES), 1)
    kk = lax.broadcasted_iota(jnp.int32, (blk, blk), 0)
    qq = lax.broadcasted_iota(jnp.int32, (blk, blk), 1)
    causal = kk <= qq
    kk_f = kk.astype(F32)
    rowj = lax.broadcasted_iota(jnp.int32, (KMEAN_ROWS, blk), 0)
    orow = lax.broadcasted_iota(jnp.int32, (V7X_LANES, blk), 0)

    for i in range(n_blk):
        n = (i + 1) * blk
        q_i = q_ref[0, i * blk:(i + 1) * blk, :]
        k_n = k_ref[0, 0:n, :]
        vt_n = vt_ref[0, :, 0:n]
        out_t = None
        for h in range(HEADS_PER_STEP):
            slope = slope_ref[hp * HEADS_PER_STEP + h]
            in_head = (lane >= h * HEAD_DIM) & (lane < (h + 1) * HEAD_DIM)
            qh = jnp.where(in_head, q_i, jnp.zeros_like(q_i))
            s_t = lax.dot_general(k_n, qh, _NT, preferred_element_type=F32)

            if i > MOBA_TOPK:
                valid = rowj < i
                g_t = lax.dot_general(km, qh, _NT, preferred_element_type=F32)
                gm = jnp.where(valid, g_t, NEG)
                rank = jnp.zeros((KMEAN_ROWS, blk), jnp.int32)
                for jj in range(i):
                    gj = gm[jj:jj + 1, :]
                    beats = (gj > gm) | ((gj == gm) & (rowj > jj))
                    rank = rank + beats.astype(jnp.int32)
                sel = valid & (rank < MOBA_TOPK)
                row_add = jnp.where(sel, slope * ((rowj - i) * blk).astype(F32), NEG)
                past_add = [row_add[j:j + 1, :] for j in range(i)]
            else:
                past_add = [slope * float((j - i) * blk) for j in range(i)]
            bias_kk = slope * kk_f

            tiles = []
            for j in range(i + 1):
                t = s_t[j * blk:(j + 1) * blk, :] + bias_kk
                if j == i:
                    t = jnp.where(causal, t, NEG)
                else:
                    t = t + past_add[j]
                tiles.append(t)
            m = tiles[0]
            for t in tiles[1:]:
                m = jnp.maximum(m, t)
            m = jnp.max(m, axis=0, keepdims=True)
            l = jnp.zeros((1, blk), F32)
            p_tiles = []
            for t in tiles:
                p = jnp.exp(t - m)
                l = l + jnp.sum(p, axis=0, keepdims=True)
                p_tiles.append(p.astype(BF16))
            p_t = p_tiles[0] if i == 0 else jnp.concatenate(p_tiles, axis=0)
            o_t = _dot(vt_n, p_t) / l
            head_rows = (orow >= h * HEAD_DIM) & (orow < (h + 1) * HEAD_DIM)
            out_t = o_t if out_t is None else jnp.where(head_rows, o_t, out_t)
        o_ref[0, i * blk:(i + 1) * blk, :] = out_t.T


def _moba(slopes, q, k, vt, km):
    b, s, d = q.shape
    n_blk = s // MOBA_BLOCK
    kern = functools.partial(_moba_kernel, n_blk=n_blk)
    return pl.pallas_call(
        kern,
        grid=(b, d // V7X_LANES),
        in_specs=[
            pl.BlockSpec(memory_space=pltpu.SMEM),
            pl.BlockSpec((1, s, V7X_LANES), lambda i, p: (i, 0, p)),
            pl.BlockSpec((1, s, V7X_LANES), lambda i, p: (i, 0, p)),
            pl.BlockSpec((1, V7X_LANES, s), lambda i, p: (i, p, 0)),
            pl.BlockSpec((1, KMEAN_ROWS, V7X_LANES), lambda i, p: (i, 0, p)),
        ],
        out_specs=pl.BlockSpec((1, s, V7X_LANES), lambda i, p: (i, 0, p)),
        out_shape=jax.ShapeDtypeStruct((b, s, d), F32),
        compiler_params=pltpu.CompilerParams(
            dimension_semantics=("arbitrary", "arbitrary"),
            vmem_limit_bytes=56 * 1024 * 1024),
        name="moba_attn",
    )(slopes, q, k, vt, km)


def _attn_out_kernel(a_ref, z_ref, h_ref, wout_ref, o_ref):
    z = z_ref[0]
    y = a_ref[0] * (z * jax.nn.sigmoid(z))
    o_ref[0] = h_ref[0] + _dot(y.astype(BF16), wout_ref[...])


def _attn_out(a, z, h, w_out, *, tm=512):
    b, s, d = h.shape
    row = lambda i, t: (i, t, 0)
    return pl.pallas_call(
        _attn_out_kernel,
        grid=(b, s // tm),
        in_specs=[
            pl.BlockSpec((1, tm, d), row),
            pl.BlockSpec((1, tm, d), row),
            pl.BlockSpec((1, tm, d), row),
            pl.BlockSpec((d, d), lambda i, t: (0, 0)),
        ],
        out_specs=pl.BlockSpec((1, tm, d), row),
        out_shape=jax.ShapeDtypeStruct((b, s, d), F32),
        compiler_params=pltpu.CompilerParams(
            dimension_semantics=("arbitrary", "arbitrary"),
            vmem_limit_bytes=56 * 1024 * 1024),
        name="attn_out",
    )(a, z, h, w_out)


def _head_mean_matrix(ce):
    idx = np.arange(ce) // HEAD_DIM
    return jnp.asarray((idx[:, None] == idx[None, :]).astype(np.float32) / HEAD_DIM, dtype=BF16)


def kernel(x, norm_g, conv_w_in, conv_w, conv_w_out, attn_w_in, q_norm_g, k_norm_g, attn_w_out):
    b, s, d = x.shape
    assert d == D_MODEL and s % MOBA_BLOCK == 0 and s // MOBA_BLOCK <= V7X_SUBLANES
    assert norm_g.shape[0] == 2, "one conv layer followed by one attention layer"

    h = _conv_layer(x, norm_g[0:1], conv_w_in[0].astype(BF16), conv_w[0],
                    conv_w_out[0].astype(BF16))

    qg = jnp.tile(q_norm_g[0], N_HEADS)[None, :]
    kg = jnp.tile(k_norm_g[0], N_HEADS)[None, :]
    ce = 256
    q, k, vt, z, km = _attn_proj(h, norm_g[1:2], attn_w_in[0].astype(BF16), qg, kg,
                                 _head_mean_matrix(ce), ce=ce)
    n_blk = s // MOBA_BLOCK
    km = jnp.pad(km.reshape(b, n_blk, d), ((0, 0), (0, KMEAN_ROWS - n_blk), (0, 0)))
    slopes = jnp.asarray(2.0 ** (-8.0 * np.arange(1, N_HEADS + 1) / N_HEADS), dtype=F32)
    a = _moba(slopes, q, k, vt, km)
    return _attn_out(a, z, h, attn_w_out[0].astype(BF16))
```

```python
import functools

import numpy as np
import jax
import jax.numpy as jnp
from jax import lax
from jax.experimental import pallas as pl
from jax.experimental.pallas import tpu as pltpu

D_MODEL = 1024
N_HEADS = 16
HEAD_DIM = D_MODEL // N_HEADS
CONV_K = 3
MOBA_BLOCK = 256
MOBA_TOPK = 3
EPS = 1e-6
NEG = -1e30

V7X_LANES = 128
V7X_SUBLANES = 8
HEADS_PER_STEP = V7X_LANES // HEAD_DIM
KMEAN_ROWS = 16
SUM_ROWS = 16
LOG2E = float(np.log2(np.e))

F32 = jnp.float32
BF16 = jnp.bfloat16

_NT = (((1,), (1,)), ((), ()))


def _rms_rows(x, g):
    ms = jnp.mean(x * x, axis=-1, keepdims=True)
    return x * lax.rsqrt(ms + EPS) * g


def _dot(a, b):
    return jnp.dot(a, b, preferred_element_type=F32)


def _conv_layer_kernel(x_ref, g_ref, win_ref, cw_ref, wout_ref, o_ref, u_scr, *, tm, ce):
    e = D_MODEL
    x = x_ref[0]
    hn = _rms_rows(x, g_ref[...]).astype(BF16)

    @pl.when(pl.program_id(1) == 0)
    def _():
        u_scr[0:V7X_SUBLANES, :] = jnp.zeros((V7X_SUBLANES, e), F32)

    def project(c):
        return [_dot(hn, win_ref[:, g * e + c * ce:g * e + (c + 1) * ce]) for g in range(4)]

    def mix(c, proj):
        lo = c * ce
        bg, cg, xv, z = proj
        u = cg * xv
        u_scr[V7X_SUBLANES:V7X_SUBLANES + tm, lo:lo + ce] = u
        um1 = u_scr[V7X_SUBLANES - 1:V7X_SUBLANES - 1 + tm, lo:lo + ce]
        um2 = u_scr[V7X_SUBLANES - 2:V7X_SUBLANES - 2 + tm, lo:lo + ce]
        conv = (cw_ref[0:1, lo:lo + ce] * um2 + cw_ref[1:2, lo:lo + ce] * um1
                + cw_ref[2:3, lo:lo + ce] * u)
        u_scr[0:V7X_SUBLANES, lo:lo + ce] = u_scr[tm:tm + V7X_SUBLANES, lo:lo + ce]
        return (bg * conv * (z * jax.nn.sigmoid(z))).astype(BF16)

    n_c = e // ce
    acc = x
    proj = {0: project(0)}
    ys = {}
    for c in range(n_c):
        if c + 1 < n_c:
            proj[c + 1] = project(c + 1)
        ys[c] = mix(c, proj.pop(c))
        if c >= 1:
            acc = acc + _dot(ys.pop(c - 1), wout_ref[(c - 1) * ce:c * ce, :])
    acc = acc + _dot(ys.pop(n_c - 1), wout_ref[(n_c - 1) * ce:n_c * ce, :])
    o_ref[0] = acc


def _conv_layer(x, g, w_in, cw, w_out, *, tm=512, ce=256):
    b, s, d = x.shape
    kern = functools.partial(_conv_layer_kernel, tm=tm, ce=ce)
    return pl.pallas_call(
        kern,
        grid=(b, s // tm),
        in_specs=[
            pl.BlockSpec((1, tm, d), lambda i, t: (i, t, 0)),
            pl.BlockSpec((1, d), lambda i, t: (0, 0)),
            pl.BlockSpec((d, 4 * d), lambda i, t: (0, 0)),
            pl.BlockSpec((CONV_K, d), lambda i, t: (0, 0)),
            pl.BlockSpec((d, d), lambda i, t: (0, 0)),
        ],
        out_specs=pl.BlockSpec((1, tm, d), lambda i, t: (i, t, 0)),
        out_shape=jax.ShapeDtypeStruct((b, s, d), F32),
        scratch_shapes=[pltpu.VMEM((tm + V7X_SUBLANES, d), F32)],
        compiler_params=pltpu.CompilerParams(
            dimension_semantics=("arbitrary", "arbitrary"),
            vmem_limit_bytes=56 * 1024 * 1024),
        name="conv_layer",
    )(x, g, w_in, cw, w_out)


def _attn_proj_kernel(h_ref, g_ref, win_ref, qg_ref, kg_ref, hm_ref,
                      q_ref, k_ref, vt_ref, z_ref, km_ref, *, tm, ce):
    e = D_MODEL
    hn = _rms_rows(h_ref[0], g_ref[...]).astype(BF16)
    hm = hm_ref[...]
    nb = tm // MOBA_BLOCK

    def project(c):
        return [_dot(hn, win_ref[:, g * e + c * ce:g * e + (c + 1) * ce]) for g in range(4)]

    def square_means(proj):
        q, k = proj[0], proj[1]
        return _dot((q * q).astype(BF16), hm), _dot((k * k).astype(BF16), hm)

    def finish(c, proj, ms):
        lo = c * ce
        q, k, v, z = proj
        qn = q * lax.rsqrt(ms[0] + EPS) * qg_ref[0:1, lo:lo + ce] * (HEAD_DIM ** -0.5 * LOG2E)
        q_ref[0, :, lo:lo + ce] = qn.astype(BF16)
        kn = k * lax.rsqrt(ms[1] + EPS) * kg_ref[0:1, lo:lo + ce]
        k_ref[0, :, lo:lo + ce] = kn.astype(BF16)
        for j in range(nb):
            km_ref[0, 0, j:j + 1, lo:lo + ce] = jnp.mean(
                kn[j * MOBA_BLOCK:(j + 1) * MOBA_BLOCK, :], axis=0, keepdims=True)
        vt_ref[0, lo:lo + ce, :] = v.T.astype(BF16)
        z_ref[0, :, lo:lo + ce] = z

    n_c = e // ce
    proj = {0: project(0)}
    for c in range(n_c):
        if c + 1 < n_c:
            proj[c + 1] = project(c + 1)
        ms = square_means(proj[c])
        finish(c, proj.pop(c), ms)


def _attn_proj(h, g, w_in, qg, kg, hm, *, tm=512, ce=256):
    b, s, d = h.shape
    nb = tm // MOBA_BLOCK
    kern = functools.partial(_attn_proj_kernel, tm=tm, ce=ce)
    row = lambda i, t: (i, t, 0)
    const = lambda i, t: (0, 0)
    return pl.pallas_call(
        kern,
        grid=(b, s // tm),
        in_specs=[
            pl.BlockSpec((1, tm, d), row),
            pl.BlockSpec((1, d), const),
            pl.BlockSpec((d, 4 * d), const),
            pl.BlockSpec((1, d), const),
            pl.BlockSpec((1, d), const),
            pl.BlockSpec((ce, ce), const),
        ],
        out_specs=[
            pl.BlockSpec((1, tm, d), row),
            pl.BlockSpec((1, tm, d), row),
            pl.BlockSpec((1, d, tm), lambda i, t: (i, 0, t)),
            pl.BlockSpec((1, tm, d), row),
            pl.BlockSpec((1, 1, nb, d), lambda i, t: (i, t, 0, 0)),
        ],
        out_shape=[
            jax.ShapeDtypeStruct((b, s, d), BF16),
            jax.ShapeDtypeStruct((b, s, d), BF16),
            jax.ShapeDtypeStruct((b, d, s), BF16),
            jax.ShapeDtypeStruct((b, s, d), F32),
            jax.ShapeDtypeStruct((b, s // tm, nb, d), F32),
        ],
        compiler_params=pltpu.CompilerParams(
            dimension_semantics=("arbitrary", "arbitrary"),
            vmem_limit_bytes=56 * 1024 * 1024),
        name="attn_proj",
    )(h, g, w_in, qg, kg, hm)


def _moba_kernel(slope_ref, q_ref, k_ref, vt_ref, km_ref, o_ref, *, n_blk):
    blk = MOBA_BLOCK
    hp = pl.program_id(1)
    km = km_ref[0].astype(BF16)
    lane = lax.broadcasted_iota(jnp.int32, (blk, V7X_LANES), 1)
    kk = lax.broadcasted_iota(jnp.int32, (blk, blk), 0)
    qq = lax.broadcasted_iota(jnp.int32, (blk, blk), 1)
    causal = kk <= qq
    kk_f = kk.astype(F32)
    rowj = lax.broadcasted_iota(jnp.int32, (KMEAN_ROWS, blk), 0)

    def scores(i, h):
        n = (i + 1) * blk
        q_i = q_ref[0, i * blk:(i + 1) * blk, :]
        slope = slope_ref[hp * HEADS_PER_STEP + h]
        in_head = (lane >= h * HEAD_DIM) & (lane < (h + 1) * HEAD_DIM)
        qh = jnp.where(in_head, q_i, jnp.zeros_like(q_i))
        s_t = lax.dot_general(k_ref[0, 0:n, :], qh, _NT, preferred_element_type=F32)
        if i > MOBA_TOPK:
            valid = rowj < i
            g_t = lax.dot_general(km, qh, _NT, preferred_element_type=F32)
            gm = jnp.where(valid, g_t, NEG)
            rank = jnp.zeros((KMEAN_ROWS, blk), jnp.int32)
            for jj in range(i):
                gj = gm[jj:jj + 1, :]
                beats = (gj > gm) | ((gj == gm) & (rowj > jj))
                rank = rank + beats.astype(jnp.int32)
            sel = valid & (rank < MOBA_TOPK)
            row_add = jnp.where(sel, slope * ((rowj - i) * blk).astype(F32), NEG)
            past_add = [row_add[j:j + 1, :] for j in range(i)]
        else:
            past_add = [slope * float((j - i) * blk) for j in range(i)]
        return dict(i=i, h=h, s_t=s_t, past_add=past_add, bias_kk=slope * kk_f)

    def masked_max(u):
        i = u["i"]
        u["tiles"] = []
        m = None
        for j in range(i + 1):
            t = u["s_t"][j * blk:(j + 1) * blk, :] + u["bias_kk"]
            if j == i:
                t = jnp.where(causal, t, NEG)
            else:
                t = t + u["past_add"][j]
            u["tiles"].append(t)
            m = t if m is None else jnp.maximum(m, t)
            yield
        u["m"] = jnp.max(m, axis=0, keepdims=True)

    def exponentials(u):
        p_tiles = []
        for t in u["tiles"]:
            p_tiles.append(jnp.exp2(t - u["m"]).astype(BF16))
            yield
        u["p_t"] = p_tiles[0] if len(p_tiles) == 1 else jnp.concatenate(p_tiles, axis=0)

    def interleave(*gens):
        live = list(gens)
        while live:
            live = [g for g in live if next(g, StopIteration) is not StopIteration]

    pending = {}

    def weighted_values(u):
        i, h = u["i"], u["h"]
        n = (i + 1) * blk
        vt_h = vt_ref[0, h * HEAD_DIM:(h + 1) * HEAD_DIM, 0:n]
        vt_aug = jnp.concatenate([vt_h, jnp.ones((SUM_ROWS, n), BF16)], axis=0)
        o_aug = _dot(vt_aug, u["p_t"])
        pending.setdefault(i, []).append(o_aug[0:HEAD_DIM, :] / o_aug[HEAD_DIM:HEAD_DIM + 1, :])
        if h == HEADS_PER_STEP - 1:
            o_ref[0, i * blk:(i + 1) * blk, :] = jnp.concatenate(pending.pop(i), axis=0).T

    units = [(i, h) for i in range(n_blk) for h in range(HEADS_PER_STEP)]
    state = {0: scores(*units[0])}
    if len(units) > 1:
        state[1] = scores(*units[1])
    interleave(masked_max(state[0]))
    for u in range(len(units)):
        if u + 2 < len(units):
            state[u + 2] = scores(*units[u + 2])
        stages = [exponentials(state[u])]
        if u + 1 < len(units):
            stages.append(masked_max(state[u + 1]))
        interleave(*stages)
        weighted_values(state.pop(u))


def _moba(slopes, q, k, vt, km):
    b, s, d = q.shape
    n_blk = s // MOBA_BLOCK
    kern = functools.partial(_moba_kernel, n_blk=n_blk)
    return pl.pallas_call(
        kern,
        grid=(b, d // V7X_LANES),
        in_specs=[
            pl.BlockSpec(memory_space=pltpu.SMEM),
            pl.BlockSpec((1, s, V7X_LANES), lambda i, p: (i, 0, p)),
            pl.BlockSpec((1, s, V7X_LANES), lambda i, p: (i, 0, p)),
            pl.BlockSpec((1, V7X_LANES, s), lambda i, p: (i, p, 0)),
            pl.BlockSpec((1, KMEAN_ROWS, V7X_LANES), lambda i, p: (i, 0, p)),
        ],
        out_specs=pl.BlockSpec((1, s, V7X_LANES), lambda i, p: (i, 0, p)),
        out_shape=jax.ShapeDtypeStruct((b, s, d), F32),
        compiler_params=pltpu.CompilerParams(
            dimension_semantics=("arbitrary", "arbitrary"),
            vmem_limit_bytes=56 * 1024 * 1024),
        name="moba_attn",
    )(slopes, q, k, vt, km)


def _attn_out_kernel(a_ref, z_ref, h_ref, wout_ref, o_ref):
    z = z_ref[0]
    y = a_ref[0] * (z * jax.nn.sigmoid(z))
    o_ref[0] = h_ref[0] + _dot(y.astype(BF16), wout_ref[...])


def _attn_out(a, z, h, w_out, *, tm=512):
    b, s, d = h.shape
    row = lambda i, t: (i, t, 0)
    return pl.pallas_call(
        _attn_out_kernel,
        grid=(b, s // tm),
        in_specs=[
            pl.BlockSpec((1, tm, d), row),
            pl.BlockSpec((1, tm, d), row),
            pl.BlockSpec((1, tm, d), row),
            pl.BlockSpec((d, d), lambda i, t: (0, 0)),
        ],
        out_specs=pl.BlockSpec((1, tm, d), row),
        out_shape=jax.ShapeDtypeStruct((b, s, d), F32),
        compiler_params=pltpu.CompilerParams(
            dimension_semantics=("arbitrary", "arbitrary"),
            vmem_limit_bytes=56 * 1024 * 1024),
        name="attn_out",
    )(a, z, h, w_out)


def _head_mean_matrix(ce):
    idx = np.arange(ce) // HEAD_DIM
    return jnp.asarray((idx[:, None] == idx[None, :]).astype(np.float32) / HEAD_DIM, dtype=BF16)


def kernel(x, norm_g, conv_w_in, conv_w, conv_w_out, attn_w_in, q_norm_g, k_norm_g, attn_w_out):
    b, s, d = x.shape
    assert d == D_MODEL and s % MOBA_BLOCK == 0 and s // MOBA_BLOCK <= V7X_SUBLANES
    assert norm_g.shape[0] == 2, "one conv layer followed by one attention layer"

    h = _conv_layer(x, norm_g[0:1], conv_w_in[0].astype(BF16), conv_w[0],
                    conv_w_out[0].astype(BF16))

    qg = jnp.tile(q_norm_g[0], N_HEADS)[None, :]
    kg = jnp.tile(k_norm_g[0], N_HEADS)[None, :]
    ce = 256
    q, k, vt, z, km = _attn_proj(h, norm_g[1:2], attn_w_in[0].astype(BF16), qg, kg,
                                 _head_mean_matrix(ce), ce=ce)
    n_blk = s // MOBA_BLOCK
    km = jnp.pad(km.reshape(b, n_blk, d), ((0, 0), (0, KMEAN_ROWS - n_blk), (0, 0)))
    slopes = jnp.asarray(2.0 ** (-8.0 * np.arange(1, N_HEADS + 1) / N_HEADS) * LOG2E, dtype=F32)
    a = _moba(slopes, q, k, vt, km)
    return _attn_out(a, z, h, attn_w_out[0].astype(BF16))
```

```python
import functools

import numpy as np
import jax
import jax.numpy as jnp
from jax import lax
from jax.experimental import pallas as pl
from jax.experimental.pallas import tpu as pltpu

D_MODEL = 1024
N_HEADS = 16
HEAD_DIM = D_MODEL // N_HEADS
CONV_K = 3
MOBA_BLOCK = 256
MOBA_TOPK = 3
EPS = 1e-6
NEG = -1e30

V7X_LANES = 128
V7X_SUBLANES = 8
HEADS_PER_STEP = V7X_LANES // HEAD_DIM
AUG_ROWS = 16
BLOCK_ROW0 = 8
SLOPE_PARTS = 3
SCORE_LOOKAHEAD = 3
SUM_ROWS = 16
LOG2E = float(np.log2(np.e))

F32 = jnp.float32
BF16 = jnp.bfloat16

_NT = (((1,), (1,)), ((), ()))


def _rms_rows(x, g):
    ms = jnp.mean(x * x, axis=-1, keepdims=True)
    return x * lax.rsqrt(ms + EPS) * g


def _dot(a, b):
    return jnp.dot(a, b, preferred_element_type=F32)


def _conv_layer_kernel(x_ref, g_ref, win_ref, cw_ref, wout_ref, o_ref, u_scr, *, tm, ce):
    e = D_MODEL
    x = x_ref[0]
    hn = _rms_rows(x, g_ref[...]).astype(BF16)

    @pl.when(pl.program_id(1) == 0)
    def _():
        u_scr[0:V7X_SUBLANES, :] = jnp.zeros((V7X_SUBLANES, e), F32)

    def project(c):
        return [_dot(hn, win_ref[:, g * e + c * ce:g * e + (c + 1) * ce]) for g in range(4)]

    def mix(c, proj):
        lo = c * ce
        bg, cg, xv, z = proj
        u = cg * xv
        u_scr[V7X_SUBLANES:V7X_SUBLANES + tm, lo:lo + ce] = u
        um1 = u_scr[V7X_SUBLANES - 1:V7X_SUBLANES - 1 + tm, lo:lo + ce]
        um2 = u_scr[V7X_SUBLANES - 2:V7X_SUBLANES - 2 + tm, lo:lo + ce]
        conv = (cw_ref[0:1, lo:lo + ce] * um2 + cw_ref[1:2, lo:lo + ce] * um1
                + cw_ref[2:3, lo:lo + ce] * u)
        u_scr[0:V7X_SUBLANES, lo:lo + ce] = u_scr[tm:tm + V7X_SUBLANES, lo:lo + ce]
        return (bg * conv * (z * jax.nn.sigmoid(z))).astype(BF16)

    n_c = e // ce
    acc = x
    proj = {0: project(0)}
    ys = {}
    for c in range(n_c):
        if c + 1 < n_c:
            proj[c + 1] = project(c + 1)
        ys[c] = mix(c, proj.pop(c))
        if c >= 1:
            acc = acc + _dot(ys.pop(c - 1), wout_ref[(c - 1) * ce:c * ce, :])
    acc = acc + _dot(ys.pop(n_c - 1), wout_ref[(n_c - 1) * ce:n_c * ce, :])
    o_ref[0] = acc


def _conv_layer(x, g, w_in, cw, w_out, *, tm=512, ce=256):
    b, s, d = x.shape
    kern = functools.partial(_conv_layer_kernel, tm=tm, ce=ce)
    return pl.pallas_call(
        kern,
        grid=(b, s // tm),
        in_specs=[
            pl.BlockSpec((1, tm, d), lambda i, t: (i, t, 0)),
            pl.BlockSpec((1, d), lambda i, t: (0, 0)),
            pl.BlockSpec((d, 4 * d), lambda i, t: (0, 0)),
            pl.BlockSpec((CONV_K, d), lambda i, t: (0, 0)),
            pl.BlockSpec((d, d), lambda i, t: (0, 0)),
        ],
        out_specs=pl.BlockSpec((1, tm, d), lambda i, t: (i, t, 0)),
        out_shape=jax.ShapeDtypeStruct((b, s, d), F32),
        scratch_shapes=[pltpu.VMEM((tm + V7X_SUBLANES, d), F32)],
        compiler_params=pltpu.CompilerParams(
            dimension_semantics=("arbitrary", "arbitrary"),
            vmem_limit_bytes=56 * 1024 * 1024),
        name="conv_layer",
    )(x, g, w_in, cw, w_out)


def _attn_proj_kernel(h_ref, g_ref, win_ref, qg_ref, kg_ref, hm_ref,
                      qt_ref, k_ref, vt_ref, z_ref, km_ref, *, tm, ce):
    e = D_MODEL
    hn = _rms_rows(h_ref[0], g_ref[...]).astype(BF16)
    hm = hm_ref[...]
    nb = tm // MOBA_BLOCK

    def project(c):
        return [_dot(hn, win_ref[:, g * e + c * ce:g * e + (c + 1) * ce]) for g in range(4)]

    def square_means(proj):
        q, k = proj[0], proj[1]
        return _dot((q * q).astype(BF16), hm), _dot((k * k).astype(BF16), hm)

    def finish(c, proj, ms):
        lo = c * ce
        q, k, v, z = proj
        qn = q * lax.rsqrt(ms[0] + EPS) * qg_ref[0:1, lo:lo + ce] * (HEAD_DIM ** -0.5 * LOG2E)
        qt_ref[0, lo:lo + ce, :] = qn.T.astype(BF16)
        kn = k * lax.rsqrt(ms[1] + EPS) * kg_ref[0:1, lo:lo + ce]
        k_ref[0, :, lo:lo + ce] = kn.astype(BF16)
        for j in range(nb):
            km_ref[0, 0, j:j + 1, lo:lo + ce] = jnp.mean(
                kn[j * MOBA_BLOCK:(j + 1) * MOBA_BLOCK, :], axis=0, keepdims=True)
        vt_ref[0, lo:lo + ce, :] = v.T.astype(BF16)
        z_ref[0, :, lo:lo + ce] = (z * jax.nn.sigmoid(z)).astype(BF16)

    n_c = e // ce
    proj = {0: project(0)}
    for c in range(n_c):
        if c + 1 < n_c:
            proj[c + 1] = project(c + 1)
        ms = square_means(proj[c])
        finish(c, proj.pop(c), ms)


def _attn_proj(h, g, w_in, qg, kg, hm, *, tm=512, ce=256):
    b, s, d = h.shape
    nb = tm // MOBA_BLOCK
    kern = functools.partial(_attn_proj_kernel, tm=tm, ce=ce)
    row = lambda i, t: (i, t, 0)
    const = lambda i, t: (0, 0)
    return pl.pallas_call(
        kern,
        grid=(b, s // tm),
        in_specs=[
            pl.BlockSpec((1, tm, d), row),
            pl.BlockSpec((1, d), const),
            pl.BlockSpec((d, 4 * d), const),
            pl.BlockSpec((1, d), const),
            pl.BlockSpec((1, d), const),
            pl.BlockSpec((ce, ce), const),
        ],
        out_specs=[
            pl.BlockSpec((1, d, tm), lambda i, t: (i, 0, t)),
            pl.BlockSpec((1, tm, d), row),
            pl.BlockSpec((1, d, tm), lambda i, t: (i, 0, t)),
            pl.BlockSpec((1, tm, d), row),
            pl.BlockSpec((1, 1, nb, d), lambda i, t: (i, t, 0, 0)),
        ],
        out_shape=[
            jax.ShapeDtypeStruct((b, d, s), BF16),
            jax.ShapeDtypeStruct((b, s, d), BF16),
            jax.ShapeDtypeStruct((b, d, s), BF16),
            jax.ShapeDtypeStruct((b, s, d), BF16),
            jax.ShapeDtypeStruct((b, s // tm, nb, d), F32),
        ],
        compiler_params=pltpu.CompilerParams(
            dimension_semantics=("arbitrary", "arbitrary"),
            vmem_limit_bytes=56 * 1024 * 1024),
        name="attn_proj",
    )(h, g, w_in, qg, kg, hm)


def _moba_kernel(slope_ref, qt_ref, k_ref, kaug_ref, vt_ref, km_ref, o_ref, *, n_blk):
    blk = MOBA_BLOCK
    hp = pl.program_id(1)
    km = km_ref[0].astype(BF16)
    qrow = lax.broadcasted_iota(jnp.int32, (V7X_LANES, blk), 0)
    kk = lax.broadcasted_iota(jnp.int32, (blk, blk), 0)
    qq = lax.broadcasted_iota(jnp.int32, (blk, blk), 1)
    causal = kk <= qq
    arow = lax.broadcasted_iota(jnp.int32, (AUG_ROWS, blk), 0)
    blk_of_row = arow - BLOCK_ROW0
    zero_rows = jnp.zeros((V7X_LANES - AUG_ROWS, blk), BF16)

    def slope_rows(h):
        head = hp * HEADS_PER_STEP + h
        parts = [slope_ref[head * SLOPE_PARTS + p] for p in range(SLOPE_PARTS)]
        rows = jnp.zeros((AUG_ROWS, blk), F32)
        for r in range(2 * SLOPE_PARTS):
            rows = jnp.where(arow == r, parts[r % SLOPE_PARTS], rows)
        return rows

    slope_rows_h = [slope_rows(h) for h in range(HEADS_PER_STEP)]

    def scores(i, h):
        n = (i + 1) * blk
        qt_i = qt_ref[0, :, i * blk:(i + 1) * blk]
        in_head = (qrow >= h * HEAD_DIM) & (qrow < (h + 1) * HEAD_DIM)
        qt_h = jnp.where(in_head, qt_i, jnp.zeros_like(qt_i))
        aug = slope_rows_h[h]
        if i > MOBA_TOPK:
            valid = (blk_of_row >= 0) & (blk_of_row < i)
            gm = jnp.where(valid, _dot(km, qt_h), NEG)
            rank = jnp.zeros((AUG_ROWS, blk), jnp.int32)
            for jj in range(i):
                gj = gm[BLOCK_ROW0 + jj:BLOCK_ROW0 + jj + 1, :]
                beats = (gj > gm) | ((gj == gm) & (blk_of_row > jj))
                rank = rank + beats.astype(jnp.int32)
            aug = aug + jnp.where(valid & (rank >= MOBA_TOPK), NEG, 0.0)
        q_aug = jnp.concatenate([qt_h, aug.astype(BF16), zero_rows], axis=0)
        k_aug = jnp.concatenate([k_ref[0, 0:n, :], kaug_ref[0:n, :]], axis=1)
        return dict(i=i, h=h, s_t=_dot(k_aug, q_aug))

    def masked_max(u):
        i = u["i"]
        u["tiles"] = []
        m = None
        for j in range(i + 1):
            t = u["s_t"][j * blk:(j + 1) * blk, :]
            if j == i:
                t = jnp.where(causal, t, NEG)
            u["tiles"].append(t)
            m = t if m is None else jnp.maximum(m, t)
            yield
        u["m"] = jnp.max(m, axis=0, keepdims=True)

    def exponentials(u):
        p_tiles = []
        for t in u["tiles"]:
            p_tiles.append(jnp.exp2(t - u["m"]).astype(BF16))
            yield
        u["p_t"] = p_tiles[0] if len(p_tiles) == 1 else jnp.concatenate(p_tiles, axis=0)

    def interleave(*gens):
        live = list(gens)
        while live:
            live = [g for g in live if next(g, StopIteration) is not StopIteration]

    pending = {}

    def weighted_values(u):
        i, h = u["i"], u["h"]
        n = (i + 1) * blk
        vt_h = vt_ref[0, h * HEAD_DIM:(h + 1) * HEAD_DIM, 0:n]
        vt_aug = jnp.concatenate([vt_h, jnp.ones((SUM_ROWS, n), BF16)], axis=0)
        o_aug = _dot(vt_aug, u["p_t"])
        pending.setdefault(i, []).append(o_aug[0:HEAD_DIM, :] / o_aug[HEAD_DIM:HEAD_DIM + 1, :])
        if h == HEADS_PER_STEP - 1:
            o_ref[0, i * blk:(i + 1) * blk, :] = jnp.concatenate(pending.pop(i), axis=0).T.astype(BF16)

    units = [(i, h) for i in range(n_blk) for h in range(HEADS_PER_STEP)]
    state = {u: scores(*units[u]) for u in range(min(SCORE_LOOKAHEAD, len(units)))}
    interleave(masked_max(state[0]))
    for u in range(len(units)):
        if u + SCORE_LOOKAHEAD < len(units):
            state[u + SCORE_LOOKAHEAD] = scores(*units[u + SCORE_LOOKAHEAD])
        stages = [exponentials(state[u])]
        if u + 1 < len(units):
            stages.append(masked_max(state[u + 1]))
        interleave(*stages)
        weighted_values(state.pop(u))


def _key_aug_table(s):
    t = np.arange(s)
    tab = np.zeros((s, V7X_LANES), np.float32)
    tab[:, 0:SLOPE_PARTS] = (t % MOBA_BLOCK)[:, None]
    tab[:, SLOPE_PARTS:2 * SLOPE_PARTS] = (t // MOBA_BLOCK * MOBA_BLOCK)[:, None]
    tab[t, BLOCK_ROW0 + t // MOBA_BLOCK] = 1.0
    return jnp.asarray(tab, dtype=BF16)


def _slope_parts():
    rest = (2.0 ** (-8.0 * np.arange(1, N_HEADS + 1) / N_HEADS) * LOG2E).astype(np.float32)
    parts = []
    for _ in range(SLOPE_PARTS):
        part = rest.astype(BF16).astype(np.float32)
        parts.append(part)
        rest = rest - part
    return jnp.asarray(np.stack(parts, axis=1).reshape(-1), dtype=F32)


def _moba(slopes, qt, k, kaug, vt, km):
    b, s, d = k.shape
    n_blk = s // MOBA_BLOCK
    kern = functools.partial(_moba_kernel, n_blk=n_blk)
    return pl.pallas_call(
        kern,
        grid=(b, d // V7X_LANES),
        in_specs=[
            pl.BlockSpec(memory_space=pltpu.SMEM),
            pl.BlockSpec((1, V7X_LANES, s), lambda i, p: (i, p, 0)),
            pl.BlockSpec((1, s, V7X_LANES), lambda i, p: (i, 0, p)),
            pl.BlockSpec((s, V7X_LANES), lambda i, p: (0, 0)),
            pl.BlockSpec((1, V7X_LANES, s), lambda i, p: (i, p, 0)),
            pl.BlockSpec((1, AUG_ROWS, V7X_LANES), lambda i, p: (i, 0, p)),
        ],
        out_specs=pl.BlockSpec((1, s, V7X_LANES), lambda i, p: (i, 0, p)),
        out_shape=jax.ShapeDtypeStruct((b, s, d), BF16),
        compiler_params=pltpu.CompilerParams(
            dimension_semantics=("arbitrary", "arbitrary"),
            vmem_limit_bytes=56 * 1024 * 1024),
        name="moba_attn",
    )(slopes, qt, k, kaug, vt, km)


def _attn_out_kernel(a_ref, gate_ref, h_ref, wout_ref, o_ref):
    y = a_ref[0].astype(F32) * gate_ref[0].astype(F32)
    o_ref[0] = h_ref[0] + _dot(y.astype(BF16), wout_ref[...])


def _attn_out(a, z, h, w_out, *, tm=512):
    b, s, d = h.shape
    row = lambda i, t: (i, t, 0)
    return pl.pallas_call(
        _attn_out_kernel,
        grid=(b, s // tm),
        in_specs=[
            pl.BlockSpec((1, tm, d), row),
            pl.BlockSpec((1, tm, d), row),
            pl.BlockSpec((1, tm, d), row),
            pl.BlockSpec((d, d), lambda i, t: (0, 0)),
        ],
        out_specs=pl.BlockSpec((1, tm, d), row),
        out_shape=jax.ShapeDtypeStruct((b, s, d), F32),
        compiler_params=pltpu.CompilerParams(
            dimension_semantics=("arbitrary", "arbitrary"),
            vmem_limit_bytes=56 * 1024 * 1024),
        name="attn_out",
    )(a, z, h, w_out)


def _head_mean_matrix(ce):
    idx = np.arange(ce) // HEAD_DIM
    return jnp.asarray((idx[:, None] == idx[None, :]).astype(np.float32) / HEAD_DIM, dtype=BF16)


def kernel(x, norm_g, conv_w_in, conv_w, conv_w_out, attn_w_in, q_norm_g, k_norm_g, attn_w_out):
    b, s, d = x.shape
    assert d == D_MODEL and s % MOBA_BLOCK == 0 and s // MOBA_BLOCK <= V7X_SUBLANES
    assert norm_g.shape[0] == 2, "one conv layer followed by one attention layer"

    h = _conv_layer(x, norm_g[0:1], conv_w_in[0].astype(BF16), conv_w[0],
                    conv_w_out[0].astype(BF16))

    qg = jnp.tile(q_norm_g[0], N_HEADS)[None, :]
    kg = jnp.tile(k_norm_g[0], N_HEADS)[None, :]
    ce = 256
    qt, k, vt, gate, km = _attn_proj(h, norm_g[1:2], attn_w_in[0].astype(BF16), qg, kg,
                                     _head_mean_matrix(ce), ce=ce)
    n_blk = s // MOBA_BLOCK
    km = jnp.pad(km.reshape(b, n_blk, d),
                 ((0, 0), (BLOCK_ROW0, AUG_ROWS - BLOCK_ROW0 - n_blk), (0, 0)))
    a = _moba(_slope_parts(), qt, k, _key_aug_table(s), vt, km)
    return _attn_out(a, gate, h, attn_w_out[0].astype(BF16))
```
